```python
import math
import jax, jax.numpy as jnp
from jax import lax
import numpy as np

D_MODEL = 1024
BATCH = 4
SEQ = 4096
DEPTH = 2

POOL_WINDOWS = (2, 4, 8, 16)
N_POOL_GROUPS = len(POOL_WINDOWS)
POOL_GROUP_DIM = D_MODEL // 8
POOL_WIDTH = N_POOL_GROUPS * POOL_GROUP_DIM

SSD_WIDTH = D_MODEL
SSD_HEAD_DIM = 64
SSD_HEADS = SSD_WIDTH // SSD_HEAD_DIM
SSD_GROUPS = 2
HEADS_PER_GROUP = SSD_HEADS // SSD_GROUPS
SSD_STATE = 128
SSD_CONV = 4
SSD_CHUNK = 128
CONV_DIM = SSD_WIDTH + 2 * SSD_GROUPS * SSD_STATE

MIX_WIDTH = POOL_WIDTH + SSD_WIDTH
IN_PROJ_DIM = POOL_WIDTH + SSD_WIDTH + CONV_DIM + SSD_HEADS

PEER_HEADS = 8
PEER_TOPK = 16
N_KEYS = 128
N_EXPERTS = N_KEYS * N_KEYS
D_KEY = 256
PEER_BLOCK = 128

N_MOD = 6
EPS = 1e-6

kernel_name = "hybrid_pool_ssd_peer_adaln"


def rmsnorm(x, g):
    xf = x.astype(jnp.float32)
    y = xf * lax.rsqrt(jnp.mean(xf * xf, axis=-1, keepdims=True) + EPS)
    return (y * g.astype(jnp.float32)).astype(x.dtype)


def modulate(h, shift, scale):
    return h * (1 + scale[:, None, :]) + shift[:, None, :]


def pool_mixer(u, pool_w, pool_b, pool_scale):
    Bsz, S, _ = u.shape
    ug = u.astype(jnp.float32).reshape(Bsz, S, N_POOL_GROUPS, POOL_GROUP_DIM)
    cs = jnp.cumsum(ug, axis=1)
    t = jnp.arange(S)
    pooled = []
    for gi, w in enumerate(POOL_WINDOWS):
        csg = cs[:, :, gi]
        cs_pad = jnp.pad(csg, ((0, 0), (w, 0), (0, 0)))
        win_sum = cs_pad[:, w:] - cs_pad[:, :S]
        count = jnp.minimum(t + 1, w).astype(jnp.float32)[None, :, None]
        pooled.append(win_sum / count)
    pooled = jnp.stack(pooled, axis=2)
    mixed = pooled - ug
    y = jnp.einsum("bsgc,gcd->bsgd", mixed, pool_w.astype(jnp.float32)) + pool_b.astype(jnp.float32)
    y = y * pool_scale.astype(jnp.float32)
    return y.reshape(Bsz, S, POOL_WIDTH).astype(u.dtype)


def ssd_chunked(xdt, a, b, c):
    Bsz, S, G, R, P = xdt.shape
    N = b.shape[-1]
    L = SSD_CHUNK
    nc = S // L
    xdt = xdt.reshape(Bsz, nc, L, G, R, P)
    b = b.reshape(Bsz, nc, L, G, N)
    c = c.reshape(Bsz, nc, L, G, N)
    a = a.reshape(Bsz, nc, L, G, R).transpose(0, 3, 4, 1, 2)
    a_cs = jnp.cumsum(a, axis=-1)

    causal = jnp.tril(jnp.ones((L, L), dtype=bool))
    seg = a_cs[..., :, None] - a_cs[..., None, :]
    decay = jnp.exp(jnp.where(causal, seg, -jnp.inf))
    cb = jnp.einsum("bclgn,bcsgn->bcgls", c, b)
    y_diag = jnp.einsum("bcgls,bgrcls,bcsgrp->bclgrp", cb, decay, xdt)

    decay_to_end = jnp.exp(a_cs[..., -1:] - a_cs)
    states = jnp.einsum("bclgn,bgrcl,bclgrp->bcgrpn", b, decay_to_end, xdt)
    chunk_decay = jnp.exp(a_cs[..., -1])

    def step(h, inp):
        st, dec = inp
        return h * dec[..., None, None] + st, h

    h0 = jnp.zeros((Bsz, G, R, P, N), jnp.float32)
    _, h_prev = lax.scan(step, h0, (jnp.moveaxis(states, 1, 0), jnp.moveaxis(chunk_decay, -1, 0)))
    y_off = jnp.einsum("bclgn,cbgrpn,bgrcl->bclgrp", c, h_prev, jnp.exp(a_cs))
    return (y_diag + y_off).reshape(Bsz, S, G, R, P)


def ssd_mixer(z, xbc, dt_raw, conv_w, conv_b, dt_bias, a_log, d_skip, ssd_norm):
    Bsz, S, _ = xbc.shape
    dtype = xbc.dtype
    xf = xbc.astype(jnp.float32)
    conv = lax.conv_general_dilated(
        xf, conv_w.astype(jnp.float32).reshape(SSD_CONV, 1, CONV_DIM),
        window_strides=(1,), padding=[(SSD_CONV - 1, 0)],
        dimension_numbers=("NWC", "WIO", "NWC"), feature_group_count=CONV_DIM)
    conv = jax.nn.silu(conv + conv_b.astype(jnp.float32))
    xs = conv[..., :SSD_WIDTH].reshape(Bsz, S, SSD_GROUPS, HEADS_PER_GROUP, SSD_HEAD_DIM)
    bs = conv[..., SSD_WIDTH:SSD_WIDTH + SSD_GROUPS * SSD_STATE].reshape(Bsz, S, SSD_GROUPS, SSD_STATE)
    cs = conv[..., SSD_WIDTH + SSD_GROUPS * SSD_STATE:].reshape(Bsz, S, SSD_GROUPS, SSD_STATE)

    dt = jax.nn.softplus(dt_raw.astype(jnp.float32) + dt_bias.astype(jnp.float32))
    dt = dt.reshape(Bsz, S, SSD_GROUPS, HEADS_PER_GROUP)
    A = -jnp.exp(a_log.astype(jnp.float32)).reshape(SSD_GROUPS, HEADS_PER_GROUP)
    y = ssd_chunked(xs * dt[..., None], dt * A, bs, cs)
    y = y + xs * d_skip.astype(jnp.float32).reshape(SSD_GROUPS, HEADS_PER_GROUP)[..., None]
    y = y.reshape(Bsz, S, SSD_WIDTH) * jax.nn.silu(z.astype(jnp.float32))
    return rmsnorm(y, ssd_norm).astype(dtype)


def peer(h, w_query, sub_keys1, sub_keys2, expert_down, expert_up):
    Bsz, S, D = h.shape
    T = Bsz * S
    hf = h.reshape(T, D)
    q = (hf @ w_query).reshape(T, PEER_HEADS, 2, D_KEY // 2)
    s1 = jnp.einsum("thk,hnk->thn", q[:, :, 0], sub_keys1).astype(jnp.float32)
    s2 = jnp.einsum("thk,hnk->thn", q[:, :, 1], sub_keys2).astype(jnp.float32)
    v1, i1 = lax.top_k(s1, PEER_TOPK)
    v2, i2 = lax.top_k(s2, PEER_TOPK)
    cand = (v1[..., :, None] + v2[..., None, :]).reshape(T, PEER_HEADS, PEER_TOPK * PEER_TOPK)
    cand_idx = (i1[..., :, None] * N_KEYS + i2[..., None, :]).reshape(T, PEER_HEADS, PEER_TOPK * PEER_TOPK)
    top_s, pos = lax.top_k(cand, PEER_TOPK)
    idx = jnp.take_along_axis(cand_idx, pos, axis=-1)
    gates = jax.nn.softmax(top_s, axis=-1).astype(h.dtype)

    nb = T // PEER_BLOCK

    def block_fn(args):
        hb, ib, gb = args
        u = expert_down[ib]
        act = jax.nn.gelu(jnp.einsum("td,thkd->thk", hb, u), approximate=False)
        v = expert_up[ib]
        return jnp.einsum("thk,thkd->td", gb * act, v)

    out = lax.map(block_fn, (hf.reshape(nb, PEER_BLOCK, D),
                             idx.reshape(nb, PEER_BLOCK, PEER_HEADS, PEER_TOPK),
                             gates.reshape(nb, PEER_BLOCK, PEER_HEADS, PEER_TOPK)))
    return out.reshape(Bsz, S, D)


def setup_inputs(seed: int = 0) -> dict:
    key = jax.random.key(seed)
    ks = jax.random.split(key, 24)

    def nrm(k, shape, s):
        return jax.random.normal(k, shape, jnp.float32) * s

    x = nrm(ks[0], (BATCH, SEQ, D_MODEL), 1.0)
    c = nrm(ks[1], (BATCH, D_MODEL), 1.0)
    w_ada = nrm(ks[2], (DEPTH, D_MODEL, N_MOD * D_MODEL), 0.5 * D_MODEL ** -0.5)
    b_ada = nrm(ks[3], (DEPTH, N_MOD * D_MODEL), 0.02)
    norm_mix = 1.0 + nrm(ks[4], (DEPTH, D_MODEL), 0.02)
    norm_ffn = 1.0 + nrm(ks[5], (DEPTH, D_MODEL), 0.02)
    w_in = nrm(ks[6], (DEPTH, D_MODEL, IN_PROJ_DIM), D_MODEL ** -0.5)
    pool_w = nrm(ks[7], (DEPTH, N_POOL_GROUPS, POOL_GROUP_DIM, POOL_GROUP_DIM), POOL_GROUP_DIM ** -0.5)
    pool_b = nrm(ks[8], (DEPTH, N_POOL_GROUPS, POOL_GROUP_DIM), 0.02)
    pool_scale = 1.0 + nrm(ks[9], (DEPTH, N_POOL_GROUPS, POOL_GROUP_DIM), 0.1)
    conv_w = nrm(ks[10], (DEPTH, SSD_CONV, CONV_DIM), SSD_CONV ** -0.5)
    conv_b = nrm(ks[11], (DEPTH, CONV_DIM), 0.02)
    dt0 = jnp.exp(jax.random.uniform(ks[12], (DEPTH, SSD_HEADS), jnp.float32,
                                     minval=math.log(1e-3), maxval=math.log(1e-1)))
    dt_bias = dt0 + jnp.log(-jnp.expm1(-dt0))
    a_log = jnp.log(jax.random.uniform(ks[13], (DEPTH, SSD_HEADS), jnp.float32, minval=1.0, maxval=16.0))
    d_skip = 1.0 + nrm(ks[14], (DEPTH, SSD_HEADS), 0.1)
    ssd_norm = 1.0 + nrm(ks[15], (DEPTH, SSD_WIDTH), 0.02)
    w_out = nrm(ks[16], (DEPTH, MIX_WIDTH, D_MODEL), MIX_WIDTH ** -0.5)
    w_query = nrm(ks[17], (DEPTH, D_MODEL, PEER_HEADS * D_KEY), D_MODEL ** -0.5)
    sub_keys1 = nrm(ks[18], (DEPTH, PEER_HEADS, N_KEYS, D_KEY // 2), (D_KEY // 2) ** -0.5)
    sub_keys2 = nrm(ks[19], (DEPTH, PEER_HEADS, N_KEYS, D_KEY // 2), (D_KEY // 2) ** -0.5)
    expert_down = nrm(ks[20], (DEPTH, N_EXPERTS, D_MODEL), D_MODEL ** -0.5)
    expert_up = nrm(ks[21], (DEPTH, N_EXPERTS, D_MODEL), PEER_HEADS ** -0.5)
    norm_final = 1.0 + nrm(ks[22], (D_MODEL,), 0.02)
    return {"x": x, "c": c, "w_ada": w_ada, "b_ada": b_ada, "norm_mix": norm_mix, "norm_ffn": norm_ffn,
            "w_in": w_in, "pool_w": pool_w, "pool_b": pool_b, "pool_scale": pool_scale,
            "conv_w": conv_w, "conv_b": conv_b, "dt_bias": dt_bias, "a_log": a_log, "d_skip": d_skip,
            "ssd_norm": ssd_norm, "w_out": w_out, "w_query": w_query, "sub_keys1": sub_keys1,
            "sub_keys2": sub_keys2, "expert_down": expert_down, "expert_up": expert_up,
            "norm_final": norm_final}


def reference(x, c, w_ada, b_ada, norm_mix, norm_ffn, w_in, pool_w, pool_b, pool_scale,
              conv_w, conv_b, dt_bias, a_log, d_skip, ssd_norm, w_out, w_query, sub_keys1,
              sub_keys2, expert_down, expert_up, norm_final):
    Bsz = x.shape[0]
    c_act = jax.nn.silu(c)
    s_z = POOL_WIDTH
    s_xbc = POOL_WIDTH + SSD_WIDTH
    s_dt = POOL_WIDTH + SSD_WIDTH + CONV_DIM
    for l in range(DEPTH):
        mod = (c_act @ w_ada[l] + b_ada[l]).reshape(Bsz, N_MOD, D_MODEL)
        shift_m, scale_m, gate_m, shift_f, scale_f, gate_f = [mod[:, i] for i in range(N_MOD)]

        h = modulate(rmsnorm(x, norm_mix[l]), shift_m, scale_m)
        proj = h @ w_in[l]
        y_pool = pool_mixer(proj[..., :s_z], pool_w[l], pool_b[l], pool_scale[l])
        y_ssd = ssd_mixer(proj[..., s_z:s_xbc], proj[..., s_xbc:s_dt], proj[..., s_dt:],
                          conv_w[l], conv_b[l], dt_bias[l], a_log[l], d_skip[l], ssd_norm[l])
        y = jnp.concatenate([y_pool, y_ssd], axis=-1) @ w_out[l]
        x = x + gate_m[:, None, :] * y

        h = modulate(rmsnorm(x, norm_ffn[l]), shift_f, scale_f)
        y = peer(h, w_query[l], sub_keys1[l], sub_keys2[l], expert_down[l], expert_up[l])
        x = x + gate_f[:, None, :] * y
    return rmsnorm(x, norm_final)
```

```python
import functools

import jax
import jax.numpy as jnp
import numpy as np
from jax import lax
from jax.experimental import pallas as pl
from jax.experimental.pallas import tpu as pltpu

F32 = jnp.float32
BF16 = jnp.bfloat16

LANES = 128
SUBLANES = 8
EPS = 1e-6
N_MOD = 6

POOL_WINDOWS = (2, 4, 8, 16)
POOL_GROUP_DIM = 128
POOL_WIDTH = 512
POOL_TAIL = 16

SSD_WIDTH = 1024
SSD_HEAD_DIM = 64
SSD_HEADS = 16
SSD_GROUPS = 2
SSD_STATE = 128
SSD_CONV = 4
SSD_CHUNK = 128
CONV_DIM = 1536
CONV_TAIL = 8
MIX_WIDTH = 1536
DT_PAD = 128
IN_PROJ_PAD = POOL_WIDTH + SSD_WIDTH + CONV_DIM + DT_PAD

PEER_HEADS = 8
PEER_TOPK = 16
N_KEYS = 128
HALF_KEY = 128
N_CAND = sum(PEER_TOPK // (a + 1) for a in range(PEER_TOPK))
N_CAND_PAD = 56

NT_DIMS = (((1,), (1,)), ((), ()))


def _dot(a, b):
    return jnp.dot(a, b, preferred_element_type=F32)


def _dot_nt(a, b):
    return lax.dot_general(a, b, NT_DIMS, preferred_element_type=F32)


def _rmsnorm(x, g):
    ms = jnp.mean(x * x, axis=-1, keepdims=True)
    return x * lax.rsqrt(ms + EPS) * g


def _split3(v):
    hi = v.astype(BF16)
    r1 = v - hi.astype(F32)
    mid = r1.astype(BF16)
    lo = (r1 - mid.astype(F32)).astype(BF16)
    return hi, mid, lo


def _ada_kernel(c_ref, w_ref, b_ref, o_ref):
    c = c_ref[...]
    ca = c * jax.nn.sigmoid(c)
    o_ref[0] = jnp.dot(ca, w_ref[0], preferred_element_type=F32,
                       precision=lax.Precision.HIGHEST) + b_ref[0]


def _ada(c_pad, w_ada, b_ada):
    depth, d, nm = w_ada.shape
    tn = 1536
    return pl.pallas_call(
        _ada_kernel,
        grid=(depth, nm // tn),
        in_specs=[
            pl.BlockSpec((c_pad.shape[0], d), lambda l, j: (0, 0)),
            pl.BlockSpec((1, d, tn), lambda l, j: (l, 0, j)),
            pl.BlockSpec((1, 1, tn), lambda l, j: (l, 0, j)),
        ],
        out_specs=pl.BlockSpec((1, c_pad.shape[0], tn), lambda l, j: (l, 0, j)),
        out_shape=jax.ShapeDtypeStruct((depth, c_pad.shape[0], nm), F32),
        compiler_params=pltpu.CompilerParams(
            dimension_semantics=("arbitrary", "arbitrary"),
            vmem_limit_bytes=40 * 1024 * 1024),
        name="ada",
    )(c_pad, w_ada, b_ada.reshape(depth, 1, nm))


def _inproj_kernel(x_ref, mod_ref, g_ref, w_ref, u_ref, z_ref, xbc_ref, dt_ref):
    y = _rmsnorm(x_ref[...], g_ref[...])
    h = (y * (1.0 + mod_ref[0, 1:2, :]) + mod_ref[0, 0:1, :]).astype(BF16)
    p = _dot(h, w_ref[...])
    s_z = POOL_WIDTH
    s_xbc = s_z + SSD_WIDTH
    s_dt = s_xbc + CONV_DIM
    u_ref[...] = p[:, :s_z]
    z_ref[...] = p[:, s_z:s_xbc]
    xbc_ref[...] = p[:, s_xbc:s_dt]
    dt_ref[...] = p[:, s_dt:]


def _inproj(xf, mod, g, w_pad, seq):
    t, d = xf.shape
    tm = 512
    tpb = seq // tm
    row = lambda i: (i, 0)
    return pl.pallas_call(
        _inproj_kernel,
        grid=(t // tm,),
        in_specs=[
            pl.BlockSpec((tm, d), row),
            pl.BlockSpec((1, N_MOD, d), lambda i: (i // tpb, 0, 0)),
            pl.BlockSpec((1, d), lambda i: (0, 0)),
            pl.BlockSpec((d, IN_PROJ_PAD), lambda i: (0, 0)),
        ],
        out_specs=[
            pl.BlockSpec((tm, POOL_WIDTH), row),
            pl.BlockSpec((tm, SSD_WIDTH), row),
            pl.BlockSpec((tm, CONV_DIM), row),
            pl.BlockSpec((tm, DT_PAD), row),
        ],
        out_shape=[
            jax.ShapeDtypeStruct((t, POOL_WIDTH), F32),
            jax.ShapeDtypeStruct((t, SSD_WIDTH), F32),
            jax.ShapeDtypeStruct((t, CONV_DIM), F32),
            jax.ShapeDtypeStruct((t, DT_PAD), F32),
        ],
        compiler_params=pltpu.CompilerParams(
            dimension_semantics=("arbitrary",),
            vmem_limit_bytes=48 * 1024 * 1024),
        name="inproj",
    )(xf, mod, g.reshape(1, d), w_pad)


def _mixer_kernel(u_ref, z_ref, xbc_ref, dt_ref, pw_ref, pb_ref, ps_ref, cw_ref,
                  cb_ref, dtb_ref, alog_ref, dexp_ref, nrm_ref, e_ref, o_ref,
                  uext_ref, cext_ref, st_ref, ys_ref):
    s = pl.program_id(1)
    L = SSD_CHUNK

    @pl.when(s == 0)
    def _init():
        uext_ref[0:POOL_TAIL, :] = jnp.zeros((POOL_TAIL, POOL_WIDTH), F32)
        cext_ref[0:CONV_TAIL, :] = jnp.zeros((CONV_TAIL, CONV_DIM), F32)
        st_ref[...] = jnp.zeros_like(st_ref)

    u = u_ref[...]
    uext_ref[POOL_TAIL:POOL_TAIL + L, :] = u
    row = lax.broadcasted_iota(jnp.int32, (L, LANES), 0)
    pos1 = (s * L + row + 1).astype(F32)
    for gi, w in enumerate(POOL_WINDOWS):
        c0 = gi * POOL_GROUP_DIM
        ug = u[:, c0:c0 + POOL_GROUP_DIM]
        win = ug
        for jj in range(1, w):
            win = win + uext_ref[POOL_TAIL - jj:POOL_TAIL - jj + L, c0:c0 + POOL_GROUP_DIM]
        mixed = win / jnp.minimum(pos1, float(w)) - ug
        y = _dot(mixed.astype(BF16), pw_ref[gi]) + pb_ref[:, c0:c0 + POOL_GROUP_DIM]
        y = y * ps_ref[:, c0:c0 + POOL_GROUP_DIM]
        o_ref[:, c0:c0 + POOL_GROUP_DIM] = y.astype(o_ref.dtype)
    uext_ref[0:POOL_TAIL, :] = u[L - POOL_TAIL:, :]

    xbc = xbc_ref[...]
    cext_ref[CONV_TAIL:CONV_TAIL + L, :] = xbc
    conv = xbc * cw_ref[SSD_CONV - 1:SSD_CONV, :]
    for jj in range(SSD_CONV - 1):
        off = CONV_TAIL - (SSD_CONV - 1) + jj
        conv = conv + cext_ref[off:off + L, :] * cw_ref[jj:jj + 1, :]
    cext_ref[0:CONV_TAIL, :] = xbc[L - CONV_TAIL:, :]
    conv = conv + cb_ref[...]
    conv = conv * jax.nn.sigmoid(conv)
    xs = conv[:, :SSD_WIDTH]
    b_all = conv[:, SSD_WIDTH:SSD_WIDTH + SSD_GROUPS * SSD_STATE]
    c_all = conv[:, SSD_WIDTH + SSD_GROUPS * SSD_STATE:]

    dtr = dt_ref[...] + dtb_ref[...]
    dt = jnp.maximum(dtr, 0.0) + jnp.log1p(jnp.exp(-jnp.abs(dtr)))
    a = dt * (-jnp.exp(alog_ref[...]))
    rr = lax.broadcasted_iota(jnp.int32, (L, L), 0)
    cc = lax.broadcasted_iota(jnp.int32, (L, L), 1)
    causal = rr >= cc
    tri = jnp.where(causal, 1.0, 0.0).astype(BF16)
    a_hi, a_mid, a_lo = _split3(a)
    a_cs = _dot(tri, a_hi) + _dot(tri, a_mid) + _dot(tri, a_lo)
    a_cs_t = a_cs.T

    e = e_ref[...]
    d_hi, d_mid, d_lo = _split3(dt)
    dt_x = _dot(d_hi, e) + _dot(d_mid, e) + _dot(d_lo, e)
    c_hi, c_mid, c_lo = _split3(a_cs)
    acs_x = _dot(c_hi, e) + _dot(c_mid, e) + _dot(c_lo, e)
    a_last = acs_x[L - 1:L, :]
    from_start = jnp.exp(acs_x)
    to_end = jnp.exp(a_last - acs_x)
    chunk_decay = jnp.exp(a_last)

    xdt = xs * dt_x
    lane = lax.broadcasted_iota(jnp.int32, (L, LANES), 1)
    left = lane < SSD_HEAD_DIM
    gw = SSD_WIDTH // SSD_GROUPS
    for g in range(SSD_GROUPS):
        bg = b_all[:, g * SSD_STATE:(g + 1) * SSD_STATE]
        cg = c_all[:, g * SSD_STATE:(g + 1) * SSD_STATE].astype(BF16)
        cb = _dot_nt(cg, bg.astype(BF16))
        bg_t = bg.T.astype(BF16)
        wg = (xdt[:, g * gw:(g + 1) * gw] * to_end[:, g * gw:(g + 1) * gw]).astype(BF16)
        s_new = _dot(bg_t, wg)
        h_prev = st_ref[g]
        y_off = _dot(cg, h_prev.astype(BF16)) * from_start[:, g * gw:(g + 1) * gw]
        st_ref[g] = h_prev * chunk_decay[:, g * gw:(g + 1) * gw] + s_new
        for jp in range(gw // LANES):
            r0 = g * (SSD_HEADS // SSD_GROUPS) + 2 * jp
            c0 = r0 * SSD_HEAD_DIM
            xp = xdt[:, c0:c0 + LANES]
            yd = None
            for r, keep in ((r0, left), (r0 + 1, jnp.logical_not(left))):
                seg = a_cs[:, r:r + 1] - a_cs_t[r:r + 1, :]
                m = (cb * jnp.exp(jnp.where(causal, seg, -jnp.inf))).astype(BF16)
                part = _dot(m, jnp.where(keep, xp, 0.0).astype(BF16))
                yd = part if yd is None else yd + part
            ys_ref[:, c0:c0 + LANES] = yd + y_off[:, jp * LANES:(jp + 1) * LANES]

    y = ys_ref[...] + xs * dexp_ref[...]
    zz = z_ref[...]
    y = y * (zz * jax.nn.sigmoid(zz))
    o_ref[:, POOL_WIDTH:] = _rmsnorm(y, nrm_ref[...]).astype(o_ref.dtype)


def _mixer(u, z, xbc, dt, pool_w, pool_b, pool_scale, conv_w, conv_b, dt_bias_pad,
           a_log_pad, d_exp, ssd_norm, expand, batch, seq):
    t = u.shape[0]
    L = SSD_CHUNK
    nc = seq // L
    row = lambda b, s: (b * nc + s, 0)
    c2 = lambda b, s: (0, 0)
    gw = SSD_WIDTH // SSD_GROUPS
    return pl.pallas_call(
        _mixer_kernel,
        grid=(batch, nc),
        in_specs=[
            pl.BlockSpec((L, POOL_WIDTH), row),
            pl.BlockSpec((L, SSD_WIDTH), row),
            pl.BlockSpec((L, CONV_DIM), row),
            pl.BlockSpec((L, DT_PAD), row),
            pl.BlockSpec(pool_w.shape, lambda b, s: (0, 0, 0)),
            pl.BlockSpec((1, POOL_WIDTH), c2),
            pl.BlockSpec((1, POOL_WIDTH), c2),
            pl.BlockSpec((SSD_CONV, CONV_DIM), c2),
            pl.BlockSpec((1, CONV_DIM), c2),
            pl.BlockSpec((1, DT_PAD), c2),
            pl.BlockSpec((1, DT_PAD), c2),
            pl.BlockSpec((1, SSD_WIDTH), c2),
            pl.BlockSpec((1, SSD_WIDTH), c2),
            pl.BlockSpec((DT_PAD, SSD_WIDTH), c2),
        ],
        out_specs=pl.BlockSpec((L, MIX_WIDTH), row),
        out_shape=jax.ShapeDtypeStruct((t, MIX_WIDTH), BF16),
        scratch_shapes=[
            pltpu.VMEM((POOL_TAIL + L, POOL_WIDTH), F32),
            pltpu.VMEM((CONV_TAIL + L, CONV_DIM), F32),
            pltpu.VMEM((SSD_GROUPS, SSD_STATE, gw), F32),
            pltpu.VMEM((L, SSD_WIDTH), F32),
        ],
        compiler_params=pltpu.CompilerParams(
            dimension_semantics=("arbitrary", "arbitrary"),
            vmem_limit_bytes=40 * 1024 * 1024),
        name="mixer",
    )(u, z, xbc, dt, pool_w, pool_b, pool_scale, conv_w, conv_b, dt_bias_pad,
      a_log_pad, d_exp, ssd_norm, expand)


def _outproj_kernel(x_ref, y_ref, w_ref, mod_ref, o_ref):
    y = _dot(y_ref[...], w_ref[...])
    o_ref[...] = x_ref[...] + mod_ref[0, 2:3, :] * y


def _outproj(xf, ymix, w_out, mod, seq):
    t, d = xf.shape
    tm = 512
    tpb = seq // tm
    row = lambda i: (i, 0)
    return pl.pallas_call(
        _outproj_kernel,
        grid=(t // tm,),
        in_specs=[
            pl.BlockSpec((tm, d), row),
            pl.BlockSpec((tm, MIX_WIDTH), row),
            pl.BlockSpec((MIX_WIDTH, d), lambda i: (0, 0)),
            pl.BlockSpec((1, N_MOD, d), lambda i: (i // tpb, 0, 0)),
        ],
        out_specs=pl.BlockSpec((tm, d), row),
        out_shape=jax.ShapeDtypeStruct((t, d), F32),
        compiler_params=pltpu.CompilerParams(
            dimension_semantics=("arbitrary",),
            vmem_limit_bytes=40 * 1024 * 1024),
        name="outproj",
    )(xf, ymix, w_out, mod)


def _top_rows(x, k):
    vals = []
    for i in range(k):
        m = jnp.max(x, axis=0, keepdims=True)
        vals.append(m)
        if i + 1 < k:
            x = jnp.where(x == m, -jnp.inf, x)
    return vals


def _peer_kernel(x_ref, mod_ref, g_ref, nf_ref, wq_ref, k1_ref, k2_ref, d_ref, ut_ref,
                 o_ref, h2_ref, q_ref, s1_ref, s2_ref, e1_ref, e2_ref, thr_ref,
                 cand_ref, st_ref, pt_ref, acc_ref, *, tm, tn, final_norm):
    j = pl.program_id(1)
    nj = pl.num_programs(1)
    n_lb = tm // LANES
    ci = tn // N_KEYS

    @pl.when(j == 0)
    def _prologue():
        y = _rmsnorm(x_ref[...], g_ref[...])
        h2 = (y * (1.0 + mod_ref[0, 4:5, :]) + mod_ref[0, 3:4, :]).astype(BF16)
        h2_ref[...] = h2
        q_ref[...] = _dot(h2, wq_ref[...]).astype(BF16)
        acc_ref[...] = jnp.zeros_like(acc_ref)

        def head_body(h, carry):
            qo = pl.multiple_of(h * 2 * HALF_KEY, 2 * HALF_KEY)
            s1_ref[h] = _dot_nt(k1_ref[h], q_ref[:, pl.ds(qo, HALF_KEY)])
            s2_ref[h] = _dot_nt(k2_ref[h], q_ref[:, pl.ds(qo + HALF_KEY, HALF_KEY)])

            def lane_body(lb, carry2):
                cols = pl.ds(pl.multiple_of(lb * LANES, LANES), LANES)
                s1b = s1_ref[h, :, cols]
                s2b = s2_ref[h, :, cols]
                v1 = _top_rows(s1b, PEER_TOPK)
                v2 = _top_rows(s2b, PEER_TOPK)
                r = 0
                for a_ in range(PEER_TOPK):
                    for b_ in range(PEER_TOPK // (a_ + 1)):
                        cand_ref[r:r + 1, :] = v1[a_] + v2[b_]
                        r += 1
                cand_ref[N_CAND:N_CAND_PAD, :] = jnp.full((N_CAND_PAD - N_CAND, LANES), -jnp.inf, F32)
                cnd = cand_ref[...]
                m0 = v1[0] + v2[0]
                zsum = jnp.zeros((1, LANES), F32)
                thr = m0
                for kk in range(PEER_TOPK):
                    m = jnp.max(cnd, axis=0, keepdims=True)
                    zsum = zsum + jnp.exp(m - m0)
                    thr = m
                    if kk + 1 < PEER_TOPK:
                        cnd = jnp.where(cnd == m, -jnp.inf, cnd)
                thr_ref[h, :, cols] = thr
                e1_ref[h, :, cols] = jnp.exp(s1b - v1[0]) * (1.0 / zsum)
                e2_ref[h, :, cols] = jnp.exp(s2b - v2[0])
                return carry2

            lax.fori_loop(0, n_lb, lane_body, 0)
            return carry

        lax.fori_loop(0, PEER_HEADS, head_body, 0)

    st_ref[...] = _dot_nt(d_ref[...], h2_ref[...])

    def lane_block_body(lb, carry):
        cols = pl.ds(pl.multiple_of(lb * LANES, LANES), LANES)
        thr = [thr_ref[h, :, cols] for h in range(PEER_HEADS)]
        for ig in range(ci // SUBLANES):
            base = pl.multiple_of(j * ci + ig * SUBLANES, SUBLANES)
            s1g = [s1_ref[h, pl.ds(base, SUBLANES), cols] for h in range(PEER_HEADS)]
            e1g = [e1_ref[h, pl.ds(base, SUBLANES), cols] for h in range(PEER_HEADS)]
            for ii in range(SUBLANES):
                r0 = (ig * SUBLANES + ii) * N_KEYS
                w = jnp.zeros((N_KEYS, LANES), F32)
                for h in range(PEER_HEADS):
                    tot = s2_ref[h, :, cols] + s1g[h][ii:ii + 1, :]
                    val = e2_ref[h, :, cols] * e1g[h][ii:ii + 1, :]
                    w = w + jnp.where(tot >= thr[h], val, 0.0)
                sb = st_ref[r0:r0 + N_KEYS, cols]
                act = 0.5 * sb * (1.0 + lax.erf(sb * np.float32(np.sqrt(0.5))))
                pt_ref[r0:r0 + N_KEYS, cols] = (w * act).astype(BF16)
        return carry

    lax.fori_loop(0, n_lb, lane_block_body, 0)
    acc_ref[...] += _dot(ut_ref[...], pt_ref[...])

    @pl.when(j == nj - 1)
    def _finish():
        out = x_ref[...] + mod_ref[0, 5:6, :] * acc_ref[...].T
        if final_norm:
            out = _rmsnorm(out, nf_ref[...])
        o_ref[...] = out


def _peer(xf, mod, g, nf, wq, k1, k2, dn, ut, seq, final_norm):
    t, d = xf.shape
    ne = dn.shape[0]
    tm = 512
    tn = 1024
    tpb = seq // tm
    body = functools.partial(_peer_kernel, tm=tm, tn=tn, final_norm=final_norm)
    return pl.pallas_call(
        body,
        grid=(t // tm, ne // tn),
        in_specs=[
            pl.BlockSpec((tm, d), lambda i, j: (i, 0)),
            pl.BlockSpec((1, N_MOD, d), lambda i, j: (i // tpb, 0, 0)),
            pl.BlockSpec((1, d), lambda i, j: (0, 0)),
            pl.BlockSpec((1, d), lambda i, j: (0, 0)),
            pl.BlockSpec(wq.shape, lambda i, j: (0, 0)),
            pl.BlockSpec(k1.shape, lambda i, j: (0, 0, 0)),
            pl.BlockSpec(k2.shape, lambda i, j: (0, 0, 0)),
            pl.BlockSpec((tn, d), lambda i, j: (j, 0)),
            pl.BlockSpec((d, tn), lambda i, j: (0, j)),
        ],
        out_specs=pl.BlockSpec((tm, d), lambda i, j: (i, 0)),
        out_shape=jax.ShapeDtypeStruct((t, d), F32),
        scratch_shapes=[
            pltpu.VMEM((tm, d), BF16),
            pltpu.VMEM((tm, wq.shape[1]), BF16),
            pltpu.VMEM((PEER_HEADS, N_KEYS, tm), F32),
            pltpu.VMEM((PEER_HEADS, N_KEYS, tm), F32),
            pltpu.VMEM((PEER_HEADS, N_KEYS, tm), F32),
            pltpu.VMEM((PEER_HEADS, N_KEYS, tm), F32),
            pltpu.VMEM((PEER_HEADS, 1, tm), F32),
            pltpu.VMEM((N_CAND_PAD, LANES), F32),
            pltpu.VMEM((tn, tm), F32),
            pltpu.VMEM((tn, tm), BF16),
            pltpu.VMEM((d, tm), F32),
        ],
        compiler_params=pltpu.CompilerParams(
            dimension_semantics=("arbitrary", "arbitrary"),
            vmem_limit_bytes=56 * 1024 * 1024),
        name="peer",
    )(xf, mod, g.reshape(1, d), nf.reshape(1, d), wq, k1, k2, dn, ut)


def kernel(x, c, w_ada, b_ada, norm_mix, norm_ffn, w_in, pool_w, pool_b, pool_scale, conv_w, conv_b, dt_bias, a_log, d_skip, ssd_norm, w_out, w_query, sub_keys1, sub_keys2, expert_down, expert_up, norm_final):
    batch, seq, d = x.shape
    depth = w_ada.shape[0]
    t = batch * seq
    xf = x.reshape(t, d)

    c_pad = jnp.pad(c, ((0, 8 - batch), (0, 0)))
    mod_all = _ada(c_pad, w_ada, b_ada)[:, :batch].reshape(depth, batch, N_MOD, d)

    expand = (jnp.arange(SSD_WIDTH)[None, :] // SSD_HEAD_DIM == jnp.arange(DT_PAD)[:, None]).astype(BF16)
    pad_h = DT_PAD - SSD_HEADS

    for l in range(depth):
        mod = mod_all[l]
        w_pad = jnp.pad(w_in[l], ((0, 0), (0, pad_h))).astype(BF16)
        u, z, xbc, dt = _inproj(xf, mod, norm_mix[l], w_pad, seq)
        ymix = _mixer(
            u, z, xbc, dt,
            pool_w[l].astype(BF16),
            pool_b[l].reshape(1, POOL_WIDTH),
            pool_scale[l].reshape(1, POOL_WIDTH),
            conv_w[l],
            conv_b[l].reshape(1, CONV_DIM),
            jnp.pad(dt_bias[l], (0, pad_h)).reshape(1, DT_PAD),
            jnp.pad(a_log[l], (0, pad_h)).reshape(1, DT_PAD),
            jnp.repeat(d_skip[l], SSD_HEAD_DIM).reshape(1, SSD_WIDTH),
            ssd_norm[l].reshape(1, SSD_WIDTH),
            expand, batch, seq)
        x1 = _outproj(xf, ymix, w_out[l].astype(BF16), mod, seq)
        xf = _peer(
            x1, mod, norm_ffn[l], norm_final,
            w_query[l].astype(BF16),
            sub_keys1[l].astype(BF16),
            sub_keys2[l].astype(BF16),
            expert_down[l].astype(BF16),
            expert_up[l].T.astype(BF16),
            seq, final_norm=(l == depth - 1))
    return xf.reshape(batch, seq, d)
```
